```python
import math
import jax, jax.numpy as jnp
from jax import lax
import numpy as np

D_MODEL = 1024
BATCH = 4
SEQ = 8192
DEPTH = 2

N_A_LAYERS = DEPTH // 2
N_B_LAYERS = DEPTH - N_A_LAYERS
HEAD_DIM = 64
A_HEADS = D_MODEL // (2 * HEAD_DIM)
A_VDIM = 2 * HEAD_DIM
A_QK_WIDTH = A_HEADS * 2 * HEAD_DIM
A_V_WIDTH = A_HEADS * A_VDIM
B_HEADS = D_MODEL // HEAD_DIM
B_WIDTH = B_HEADS * HEAD_DIM
FFN_HIDDEN = 256 * ((8 * D_MODEL // 3 + 255) // 256)
BLOCK_Q = 128
NUM_BUCKETS = 32
MAX_DISTANCE = 128
N_MOD = 9
EPS = 1e-6

kernel_name = "yoco_diffattn_fox_macaron_adaln"


def rms_norm(x, g):
    xf = x.astype(jnp.float32)
    y = xf * lax.rsqrt(jnp.mean(xf * xf, axis=-1, keepdims=True) + EPS)
    return (y * g.astype(jnp.float32)).astype(x.dtype)


def modulate(h, shift, scale):
    return h * (1 + scale[:, None, :]) + shift[:, None, :]


def swiglu(h, w_in, w_out):
    g, u = jnp.split(h @ w_in, 2, axis=-1)
    return (jax.nn.silu(g) * u) @ w_out


def t5_bucket(rel):
    n = jnp.maximum(rel, 0)
    max_exact = NUM_BUCKETS // 2
    nf = jnp.maximum(n, 1).astype(jnp.float32)
    large = max_exact + (jnp.log(nf / max_exact) / math.log(MAX_DISTANCE / max_exact)
                         * (NUM_BUCKETS - max_exact)).astype(jnp.int32)
    large = jnp.minimum(large, NUM_BUCKETS - 1)
    return jnp.where(n < max_exact, n, large)


def to_blocks(t):
    b, s = t.shape[:2]
    t = t.reshape((b, s // BLOCK_Q, BLOCK_Q) + t.shape[2:])
    return jnp.moveaxis(t, 1, 0)


def from_blocks(t):
    t = jnp.moveaxis(t, 0, 1)
    return t.reshape((t.shape[0], t.shape[1] * t.shape[2]) + t.shape[3:])


def lambda_init_fn(layer):
    return 0.8 - 0.6 * math.exp(-0.3 * layer)


def diff_attention(h, w_qkv, w_o, lam_params, subln_g, rel_bias, lambda_init):
    b, s, _ = h.shape
    qkv = h @ w_qkv
    q, k, v = jnp.split(qkv, [A_QK_WIDTH, 2 * A_QK_WIDTH], axis=-1)
    q = q.reshape(b, s, A_HEADS, 2, HEAD_DIM)
    k = k.reshape(b, s, A_HEADS, 2, HEAD_DIM)
    v = v.reshape(b, s, A_HEADS, A_VDIM)
    lf = lam_params.astype(jnp.float32)
    lam = jnp.exp(jnp.sum(lf[0] * lf[1])) - jnp.exp(jnp.sum(lf[2] * lf[3])) + lambda_init
    table = rel_bias.astype(jnp.float32)
    k_pos = jnp.arange(s)
    scale = HEAD_DIM ** -0.5

    def block(args):
        qb, blk = args
        q_pos = blk * BLOCK_Q + jnp.arange(BLOCK_Q)
        rel = q_pos[:, None] - k_pos[None, :]
        bias = jnp.take(table, t5_bucket(rel), axis=0)
        bias = jnp.transpose(bias, (2, 0, 1))[None, :, None]
        logits = jnp.einsum('bqhmd,bkhmd->bhmqk', qb, k).astype(jnp.float32) * scale + bias
        logits = jnp.where(rel >= 0, logits, -jnp.inf)
        p = jax.nn.softmax(logits, axis=-1)
        w = p[:, :, 0] - lam * p[:, :, 1]
        return jnp.einsum('bhqk,bkhe->bqhe', w.astype(v.dtype), v)

    nb = s // BLOCK_Q
    o = from_blocks(lax.map(block, (to_blocks(q), jnp.arange(nb))))
    o = rms_norm(o, subln_g) * (1 - lambda_init)
    return o.reshape(b, s, A_V_WIDTH) @ w_o


def shared_kv(x, c_act, kv_ada_w, kv_ada_b, kv_norm_g, kv_w, fgate_w, fgate_b):
    b, s, _ = x.shape
    shift, scale = jnp.split(c_act @ kv_ada_w + kv_ada_b, 2, axis=-1)
    h = modulate(rms_norm(x, kv_norm_g), shift, scale)
    k, v = jnp.split(h @ kv_w, 2, axis=-1)
    k = k.reshape(b, s, B_HEADS, HEAD_DIM)
    v = v.reshape(b, s, B_HEADS, HEAD_DIM)
    log_f = jax.nn.log_sigmoid((h @ fgate_w + fgate_b).astype(jnp.float32))
    F = jnp.cumsum(log_f, axis=1)
    return k, v, F


def forgetting_attention(h, w_q, w_o, k, v, F):
    b, s, _ = h.shape
    q = (h @ w_q).reshape(b, s, B_HEADS, HEAD_DIM)
    k_pos = jnp.arange(s)
    F_k = jnp.transpose(F, (0, 2, 1))
    scale = HEAD_DIM ** -0.5

    def block(args):
        qb, Fqb, blk = args
        q_pos = blk * BLOCK_Q + jnp.arange(BLOCK_Q)
        causal = q_pos[:, None] >= k_pos[None, :]
        decay = jnp.transpose(Fqb, (0, 2, 1))[..., None] - F_k[:, :, None, :]
        logits = jnp.einsum('bqhd,bkhd->bhqk', qb, k).astype(jnp.float32) * scale + decay
        logits = jnp.where(causal, logits, -jnp.inf)
        p = jax.nn.softmax(logits, axis=-1)
        return jnp.einsum('bhqk,bkhd->bqhd', p.astype(v.dtype), v)

    nb = s // BLOCK_Q
    o = from_blocks(lax.map(block, (to_blocks(q), to_blocks(F), jnp.arange(nb))))
    return o.reshape(b, s, B_WIDTH) @ w_o


def setup_inputs(seed: int = 0) -> dict:
    key = jax.random.key(seed)
    ks = jax.random.split(key, 21)
    f32 = jnp.float32
    D = D_MODEL

    def nrm(k, shape, std):
        return jax.random.normal(k, shape, f32) * std

    return {
        "x": nrm(ks[0], (BATCH, SEQ, D), 1.0),
        "c": nrm(ks[1], (BATCH, D), 1.0),
        "ada_w": nrm(ks[2], (DEPTH, D, N_MOD * D), 0.5 * D ** -0.5),
        "ada_b": nrm(ks[3], (DEPTH, N_MOD * D), 0.02),
        "norm_g": 1.0 + nrm(ks[4], (DEPTH, 3, D), 0.02),
        "ffn_w_in": nrm(ks[5], (DEPTH, 2, D, 2 * FFN_HIDDEN), D ** -0.5),
        "ffn_w_out": nrm(ks[6], (DEPTH, 2, FFN_HIDDEN, D), FFN_HIDDEN ** -0.5),
        "a_w_qkv": nrm(ks[7], (N_A_LAYERS, D, 2 * A_QK_WIDTH + A_V_WIDTH), D ** -0.5),
        "a_w_o": nrm(ks[8], (N_A_LAYERS, A_V_WIDTH, D), A_V_WIDTH ** -0.5),
        "a_lambda": nrm(ks[9], (N_A_LAYERS, 4, HEAD_DIM), 0.1),
        "a_subln_g": 1.0 + nrm(ks[10], (N_A_LAYERS, A_VDIM), 0.02),
        "rel_bias": nrm(ks[11], (NUM_BUCKETS, A_HEADS), 0.5),
        "kv_ada_w": nrm(ks[12], (D, 2 * D), 0.5 * D ** -0.5),
        "kv_ada_b": nrm(ks[13], (2 * D,), 0.02),
        "kv_norm_g": 1.0 + nrm(ks[14], (D,), 0.02),
        "kv_w": nrm(ks[15], (D, 2 * B_WIDTH), D ** -0.5),
        "fgate_w": nrm(ks[16], (D, B_HEADS), D ** -0.5),
        "fgate_b": jax.random.uniform(ks[17], (B_HEADS,), f32, minval=1.0, maxval=4.0),
        "b_w_q": nrm(ks[18], (N_B_LAYERS, D, B_WIDTH), D ** -0.5),
        "b_w_o": nrm(ks[19], (N_B_LAYERS, B_WIDTH, D), B_WIDTH ** -0.5),
        "final_g": 1.0 + nrm(ks[20], (D,), 0.02),
    }


def reference(x, c, ada_w, ada_b, norm_g, ffn_w_in, ffn_w_out, a_w_qkv, a_w_o, a_lambda,
              a_subln_g, rel_bias, kv_ada_w, kv_ada_b, kv_norm_g, kv_w, fgate_w, fgate_b,
              b_w_q, b_w_o, final_g):
    c_act = jax.nn.silu(c)
    k_sh = v_sh = F_sh = None
    for layer in range(DEPTH):
        if layer == N_A_LAYERS:
            k_sh, v_sh, F_sh = shared_kv(x, c_act, kv_ada_w, kv_ada_b, kv_norm_g, kv_w,
                                         fgate_w, fgate_b)
        mod = c_act @ ada_w[layer] + ada_b[layer]
        sh1, sc1, g1, sh2, sc2, g2, sh3, sc3, g3 = jnp.split(mod, N_MOD, axis=-1)
        h = modulate(rms_norm(x, norm_g[layer, 0]), sh1, sc1)
        x = x + 0.5 * g1[:, None, :] * swiglu(h, ffn_w_in[layer, 0], ffn_w_out[layer, 0])
        h = modulate(rms_norm(x, norm_g[layer, 1]), sh2, sc2)
        if layer < N_A_LAYERS:
            mix = diff_attention(h, a_w_qkv[layer], a_w_o[layer], a_lambda[layer],
                                 a_subln_g[layer], rel_bias, lambda_init_fn(layer))
        else:
            j = layer - N_A_LAYERS
            mix = forgetting_attention(h, b_w_q[j], b_w_o[j], k_sh, v_sh, F_sh)
        x = x + g2[:, None, :] * mix
        h = modulate(rms_norm(x, norm_g[layer, 2]), sh3, sc3)
        x = x + 0.5 * g3[:, None, :] * swiglu(h, ffn_w_in[layer, 1], ffn_w_out[layer, 1])
    return rms_norm(x, final_g)
```

```python
import functools
import math

import numpy as np
import jax
import jax.numpy as jnp
from jax import lax
from jax.experimental import pallas as pl
from jax.experimental.pallas import tpu as pltpu

_F32 = jnp.float32
_BF16 = jnp.bfloat16

_HEAD_DIM = 64
_NUM_BUCKETS = 32
_MAX_DISTANCE = 128
_N_MOD = 9
_EPS = 1e-6
_NEG = -1e30

_V7X_VMEM_LIMIT = 56 * 1024 * 1024
_TOKEN_TILE = 512
_FFN_CHUNKS = 2
_ATTN_TILE = 256
_MOD_TILE_N = 1024


def _lambda_init(layer):
    return 0.8 - 0.6 * math.exp(-0.3 * layer)


def _bucket_lower_bounds():
    n = np.arange(0, 4 * _MAX_DISTANCE)
    max_exact = _NUM_BUCKETS // 2
    nf = np.maximum(n, 1).astype(np.float32)
    large = max_exact + (np.log(nf / np.float32(max_exact)) / np.float32(math.log(_MAX_DISTANCE / max_exact))
                         * np.float32(_NUM_BUCKETS - max_exact)).astype(np.int32)
    large = np.minimum(large, _NUM_BUCKETS - 1)
    bucket = np.where(n < max_exact, n, large)
    assert np.all(np.diff(bucket) >= 0)
    lows = [int(n[bucket == b].min()) for b in range(_NUM_BUCKETS)]
    return lows


_BUCKET_LO = _bucket_lower_bounds()


def _params(semantics):
    return pltpu.CompilerParams(dimension_semantics=semantics, vmem_limit_bytes=_V7X_VMEM_LIMIT)


def _resident(block_shape, index_map):
    return pl.BlockSpec(block_shape, index_map, pipeline_mode=pl.Buffered(1))


def _rms_mod(x, g, shift, scale):
    ms = jnp.mean(x * x, axis=-1, keepdims=True)
    y = x * lax.rsqrt(ms + _EPS) * g
    return y * (1.0 + scale) + shift


def _mod_body(c_ref, w_ref, b_ref, o_ref):
    c = c_ref[...]
    ca = (c * jax.nn.sigmoid(c)).astype(_BF16)
    o_ref[...] = jnp.dot(ca, w_ref[...].astype(_BF16), preferred_element_type=_F32) + b_ref[...]


def _mod_call(c, w, b):
    n_layers, d, n = w.shape
    bsz = c.shape[0]
    tn = _MOD_TILE_N
    return pl.pallas_call(
        _mod_body,
        grid=(n_layers, n // tn),
        in_specs=[
            pl.BlockSpec((bsz, d), lambda l, j: (0, 0)),
            pl.BlockSpec((None, d, tn), lambda l, j: (l, 0, j)),
            pl.BlockSpec((None, 1, tn), lambda l, j: (l, 0, j)),
        ],
        out_specs=pl.BlockSpec((None, bsz, tn), lambda l, j: (l, 0, j)),
        out_shape=jax.ShapeDtypeStruct((n_layers, bsz, n), _F32),
        compiler_params=_params(("parallel", "parallel")),
    )(c, w, b.reshape(n_layers, 1, n))


def _vec_spec(row_of_batch, d):
    return pl.BlockSpec((None, 1, d), lambda b, i: (row_of_batch(b), 0, 0))


def _ffn_body(*refs, has_mix, has_final, n_chunks):
    it = iter(refs)
    x_ref = next(it)
    if has_mix:
        o_ref, wo_ref, gmix_ref = next(it), next(it), next(it)
    g_ref, sh_ref, sc_ref, gate_ref, win_ref, wout_ref = (next(it) for _ in range(6))
    if has_final:
        fg_ref = next(it)
    out_ref = next(it)

    x = x_ref[...]
    if has_mix:
        x = x + gmix_ref[...] * jnp.dot(o_ref[...], wo_ref[...], preferred_element_type=_F32)
    h = _rms_mod(x, g_ref[...], sh_ref[...], sc_ref[...]).astype(_BF16)
    hidden = wout_ref.shape[0]
    fk = hidden // n_chunks
    y = None
    for c in range(n_chunks):
        g = jnp.dot(h, win_ref[:, c * fk:(c + 1) * fk], preferred_element_type=_F32)
        u = jnp.dot(h, win_ref[:, hidden + c * fk:hidden + (c + 1) * fk], preferred_element_type=_F32)
        a = (g * jax.nn.sigmoid(g) * u).astype(_BF16)
        yc = jnp.dot(a, wout_ref[c * fk:(c + 1) * fk, :], preferred_element_type=_F32)
        y = yc if y is None else y + yc
    x = x + (0.5 * gate_ref[...]) * y
    if has_final:
        ms = jnp.mean(x * x, axis=-1, keepdims=True)
        x = x * lax.rsqrt(ms + _EPS) * fg_ref[...]
    out_ref[...] = x


def _ffn_call(x, mod3, mod_row, norm_g, w_in, w_out, w_idx, mix=None, final_g=None):
    bsz, s, d = x.shape
    hidden = w_out.shape[2]
    tm = _TOKEN_TILE
    l, j = w_idx
    j0 = 6 if j else 0
    tok = lambda b, i: (b, i, 0)
    in_specs = [pl.BlockSpec((None, tm, d), tok)]
    args = [x]
    if mix is not None:
        o, w_o, lo, jg = mix
        width = o.shape[2]
        in_specs += [pl.BlockSpec((None, tm, width), tok),
                     _resident((None, width, d), lambda b, i: (lo, 0, 0)),
                     _vec_spec(lambda b: mod_row(b, jg), d)]
        args += [o, w_o, mod3]
    in_specs += [
        _resident((None, 1, d), lambda b, i: (l * 3 + (2 if j else 0), 0, 0)),
        _vec_spec(lambda b: mod_row(b, j0), d),
        _vec_spec(lambda b: mod_row(b, j0 + 1), d),
        _vec_spec(lambda b: mod_row(b, j0 + 2), d),
        _resident((None, None, d, 2 * hidden), lambda b, i: (l, j, 0, 0)),
        _resident((None, None, hidden, d), lambda b, i: (l, j, 0, 0)),
    ]
    args += [norm_g, mod3, mod3, mod3, w_in, w_out]
    if final_g is not None:
        in_specs.append(_resident((1, d), lambda b, i: (0, 0)))
        args.append(final_g)
    body = functools.partial(_ffn_body, has_mix=mix is not None, has_final=final_g is not None,
                             n_chunks=_FFN_CHUNKS)
    return pl.pallas_call(
        body,
        grid=(bsz, s // tm),
        in_specs=in_specs,
        out_specs=pl.BlockSpec((None, tm, d), tok),
        out_shape=jax.ShapeDtypeStruct((bsz, s, d), _F32),
        compiler_params=_params(("parallel", "parallel")),
    )(*args)


def _nt_dot(a, b):
    return lax.dot_general(a, b, (((1,), (1,)), ((), ())), preferred_element_type=_F32)


def _proj_body(*refs, transposed, with_gate, tk):
    it = iter(refs)
    x_ref, g_ref, sh_ref, sc_ref = (next(it) for _ in range(4))
    w_refs = [next(it) for _ in transposed]
    if with_gate:
        fgw_ref, fgb_ref = next(it), next(it)
    o_refs = [next(it) for _ in transposed]
    if with_gate:
        ft_ref = next(it)
        carry_ref = next(it)

    h = _rms_mod(x_ref[...], g_ref[...], sh_ref[...], sc_ref[...]).astype(_BF16)
    tm = h.shape[0]
    for w_ref, o_ref, t in zip(w_refs, o_refs, transposed):
        if t:
            r = _nt_dot(w_ref[...], h).astype(o_ref.dtype)
            for c in range(tm // tk):
                o_ref[c] = r[:, c * tk:(c + 1) * tk]
        else:
            o_ref[...] = jnp.dot(h, w_ref[...], preferred_element_type=_F32).astype(o_ref.dtype)

    if with_gate:
        @pl.when(pl.program_id(1) == 0)
        def _():
            carry_ref[...] = jnp.zeros_like(carry_ref)

        z = _nt_dot(fgw_ref[...], h) + fgb_ref[...]
        lf = jnp.minimum(z, 0.0) - jnp.log1p(jnp.exp(-jnp.abs(z)))
        hi = lf.astype(_BF16)
        r1 = lf - hi.astype(_F32)
        mid = r1.astype(_BF16)
        low = (r1 - mid.astype(_F32)).astype(_BF16)
        pieces = jnp.concatenate([hi, mid, low], axis=0)
        row = lax.broadcasted_iota(jnp.int32, (tm, tm), 0)
        col = lax.broadcasted_iota(jnp.int32, (tm, tm), 1)
        tri = jnp.where(row <= col, 1.0, 0.0).astype(_BF16)
        cs = jnp.dot(pieces, tri, preferred_element_type=_F32)
        nh = lf.shape[0]
        cum = cs[0:nh] + cs[nh:2 * nh] + cs[2 * nh:3 * nh]
        ft_ref[...] = cum + carry_ref[...]
        carry_ref[...] = carry_ref[...] + jnp.sum(lf, axis=1, keepdims=True)


def _proj_call(x, norm_g_spec, norm_g, mod3, shift_row, scale_row, weights, transposed, gate=None):
    bsz, s, d = x.shape
    tm, tk = _TOKEN_TILE, _ATTN_TILE
    tok = lambda b, i: (b, i, 0)
    in_specs = [pl.BlockSpec((None, tm, d), tok), norm_g_spec,
                _vec_spec(shift_row, d), _vec_spec(scale_row, d)]
    args = [x, norm_g, mod3, mod3]
    out_specs, out_shapes = [], []
    for w, t in zip(weights, transposed):
        in_specs.append(_resident(w.shape, lambda b, i: (0, 0)))
        args.append(w)
        if t:
            n = w.shape[0]
            out_specs.append(pl.BlockSpec((None, tm // tk, n, tk), lambda b, i: (b, i, 0, 0)))
            out_shapes.append(jax.ShapeDtypeStruct((bsz, s // tk, n, tk), _BF16))
        else:
            n = w.shape[1]
            out_specs.append(pl.BlockSpec((None, tm, n), tok))
            out_shapes.append(jax.ShapeDtypeStruct((bsz, s, n), _BF16))
    scratch = []
    if gate is not None:
        fgw_t, fgb = gate
        nh = fgw_t.shape[0]
        in_specs += [_resident(fgw_t.shape, lambda b, i: (0, 0)), _resident(fgb.shape, lambda b, i: (0, 0))]
        args += [fgw_t, fgb]
        out_specs.append(pl.BlockSpec((None, nh, tm), lambda b, i: (b, 0, i)))
        out_shapes.append(jax.ShapeDtypeStruct((bsz, nh, s), _F32))
        scratch.append(pltpu.VMEM((nh, 1), _F32))
    body = functools.partial(_proj_body, transposed=tuple(transposed), with_gate=gate is not None, tk=tk)
    return pl.pallas_call(
        body,
        grid=(bsz, s // tm),
        in_specs=in_specs,
        out_specs=out_specs,
        out_shape=out_shapes,
        scratch_shapes=scratch,
        compiler_params=_params(("parallel", "arbitrary" if gate is not None else "parallel")),
    )(*args)


def _attn_body(*refs, diff, lambda_init):
    it = iter(refs)
    qt_ref, k_ref, vt_ref = next(it), next(it), next(it)
    if diff:
        tab_ref, lam_ref, gcol_ref = next(it), next(it), next(it)
    else:
        f_ref = next(it)
    out_ref = next(it)
    q2_ref, m_ref, l_ref, acc_ref = next(it), next(it), next(it), next(it)
    if diff:
        bias_ref = next(it)

    g = pl.program_id(1)
    i = pl.program_id(2)
    t = qt_ref.shape[1]
    half = _HEAD_DIM

    if diff:
        @pl.when(i == 0)
        def _():
            q_off = lax.broadcasted_iota(jnp.int32, (t, t), 1)
            k_off = lax.broadcasted_iota(jnp.int32, (t, t), 0)
            last = tab_ref[_NUM_BUCKETS - 1, g]
            for which, shift in ((0, t), (1, 0)):
                dist = q_off - k_off + shift
                val = jnp.full((t, t), tab_ref[0, g] - last, _F32)
                for b in range(1, _NUM_BUCKETS - 1):
                    val = jnp.where(dist >= _BUCKET_LO[b], tab_ref[b, g] - last, val)
                val = jnp.where(dist >= _BUCKET_LO[_NUM_BUCKETS - 1], 0.0, val)
                if which == 1:
                    val = jnp.where(dist >= 0, val, _NEG)
                bias_ref[which] = val

    q = qt_ref[...]
    row = lax.broadcasted_iota(jnp.int32, q.shape, 0)
    zero = jnp.zeros_like(q)
    q2_ref[:, :t] = jnp.where(row < half, q, zero)
    q2_ref[:, t:] = jnp.where(row >= half, q, zero)
    m_ref[...] = jnp.full_like(m_ref, -jnp.inf)
    l_ref[...] = jnp.zeros_like(l_ref)
    acc_ref[...] = jnp.zeros_like(acc_ref)

    if not diff:
        f_last = f_ref[pl.ds(pl.multiple_of((i + 1) * t - 8, 8), 8), :][7:8, :]

    def tile(j, bias):
        start = pl.multiple_of(j * t, t)
        kt = k_ref[pl.ds(start, t), :]
        s = jnp.dot(kt, q2_ref[...], preferred_element_type=_F32)
        if not diff:
            fcol = f_last - f_ref[pl.ds(start, t), :]
            dec = jnp.concatenate([jnp.broadcast_to(fcol[:, 0:1], (t, t)),
                                   jnp.broadcast_to(fcol[:, 1:2], (t, t))], axis=1)
            s = s + dec
        if bias is not None:
            s = s + bias
        m_old = m_ref[...]
        m_new = jnp.maximum(m_old, jnp.max(s, axis=0, keepdims=True))
        p = jnp.exp(s - m_new)
        alpha = jnp.exp(m_old - m_new)
        l_ref[...] = alpha * l_ref[...] + jnp.sum(p, axis=0, keepdims=True)
        pb = p.astype(_BF16)
        vt = vt_ref[j]
        if diff:
            pv = jnp.dot(vt, pb, preferred_element_type=_F32)
        else:
            pv = jnp.concatenate(
                [jnp.dot(vt[:half], pb[:, :t], preferred_element_type=_F32),
                 jnp.dot(vt[half:], pb[:, t:], preferred_element_type=_F32)], axis=1)
        acc_ref[...] = alpha * acc_ref[...] + pv
        m_ref[...] = m_new

    def far_tile(j, carry):
        tile(j, None)
        return carry

    if diff:
        lax.fori_loop(0, jnp.maximum(i - 1, 0), far_tile, 0)

        @pl.when(i >= 1)
        def _():
            b = bias_ref[0]
            tile(i - 1, jnp.concatenate([b, b], axis=1))

        b = bias_ref[1]
        tile(i, jnp.concatenate([b, b], axis=1))
    else:
        lax.fori_loop(0, i, far_tile, 0)
        q_off = lax.broadcasted_iota(jnp.int32, (t, t), 1)
        k_off = lax.broadcasted_iota(jnp.int32, (t, t), 0)
        mask = jnp.where(q_off >= k_off, 0.0, _NEG).astype(_F32)
        tile(i, jnp.concatenate([mask, mask], axis=1))

    inv = 1.0 / l_ref[...]
    acc = acc_ref[...]
    if diff:
        lp = lam_ref[...]
        lam = (jnp.exp(jnp.sum(lp[0:1] * lp[1:2], axis=1, keepdims=True))
               - jnp.exp(jnp.sum(lp[2:3] * lp[3:4], axis=1, keepdims=True)) + lambda_init)
        o = acc[:, :t] * inv[:, :t] - lam * (acc[:, t:] * inv[:, t:])
        ms = jnp.mean(o * o, axis=0, keepdims=True)
        o = o * lax.rsqrt(ms + _EPS) * gcol_ref[...] * (1.0 - lambda_init)
    else:
        o = jnp.concatenate([acc[:, :t] * inv[:, :t], acc[:, t:] * inv[:, t:]], axis=0)
    out_ref[...] = o.T.astype(out_ref.dtype)


def _attn_call(qt, k, vt, *, diff, extras, lambda_init=0.0):
    bsz, s, width = k.shape
    t = _ATTN_TILE
    lanes = 2 * _HEAD_DIM
    groups = width // lanes
    in_specs = [
        pl.BlockSpec((None, None, lanes, t), lambda b, g, i: (b, i, g, 0)),
        pl.BlockSpec((None, s, lanes), lambda b, g, i: (b, 0, g)),
        pl.BlockSpec((None, s // t, lanes, t), lambda b, g, i: (b, 0, g, 0)),
    ]
    scratch = [pltpu.VMEM((lanes, 2 * t), _BF16),
               pltpu.VMEM((1, 2 * t), _F32), pltpu.VMEM((1, 2 * t), _F32)]
    if diff:
        tab, lam, gcol = extras
        in_specs += [pl.BlockSpec(memory_space=pltpu.SMEM),
                     pl.BlockSpec(lam.shape, lambda b, g, i: (0, 0)),
                     pl.BlockSpec(gcol.shape, lambda b, g, i: (0, 0))]
        scratch += [pltpu.VMEM((lanes, 2 * t), _F32), pltpu.VMEM((2, t, t), _F32)]
    else:
        (f,) = extras
        in_specs += [pl.BlockSpec((None, None, s, 2), lambda b, g, i: (b, g, 0, 0))]
        scratch += [pltpu.VMEM((_HEAD_DIM, 2 * t), _F32)]
    body = functools.partial(_attn_body, diff=diff, lambda_init=lambda_init)
    return pl.pallas_call(
        body,
        grid=(bsz, groups, s // t),
        in_specs=in_specs,
        out_specs=pl.BlockSpec((None, t, lanes), lambda b, g, i: (b, i, g)),
        out_shape=jax.ShapeDtypeStruct((bsz, s, width), _BF16),
        scratch_shapes=scratch,
        compiler_params=_params(("parallel", "parallel", "arbitrary")),
    )(qt, k, vt, *extras)


def kernel(x, c, ada_w, ada_b, norm_g, ffn_w_in, ffn_w_out, a_w_qkv, a_w_o, a_lambda, a_subln_g, rel_bias,
           kv_ada_w, kv_ada_b, kv_norm_g, kv_w, fgate_w, fgate_b, b_w_q, b_w_o, final_g):
    bsz, s, d = x.shape
    depth = ada_w.shape[0]
    n_a = a_w_qkv.shape[0]
    assert ada_w.shape[2] == _N_MOD * d and rel_bias.shape[0] == _NUM_BUCKETS
    assert a_lambda.shape[2] == _HEAD_DIM and _ATTN_TILE + 1 >= _BUCKET_LO[-1]
    assert s % _TOKEN_TILE == 0 and _TOKEN_TILE % _ATTN_TILE == 0
    scale = _HEAD_DIM ** -0.5

    mod3 = _mod_call(c, ada_w, ada_b).reshape(depth * bsz * _N_MOD, 1, d)
    kvmod3 = _mod_call(c, kv_ada_w[None], kv_ada_b[None]).reshape(bsz * 2, 1, d)

    w_in = ffn_w_in.astype(_BF16)
    w_out = ffn_w_out.astype(_BF16)
    a_wo = a_w_o.astype(_BF16)
    b_wo = b_w_o.astype(_BF16)
    norm_g3 = norm_g.reshape(depth * 3, 1, d)
    qk_w = a_w_qkv.shape[2] // 3

    for layer in range(depth):
        mod_row = lambda b, j, layer=layer: (layer * bsz + b) * _N_MOD + j
        if layer == n_a:
            kv_width = kv_w.shape[1] // 2
            k_sh, vt_sh, f_t = _proj_call(
                x, _resident((1, d), lambda b, i: (0, 0)), kv_norm_g.reshape(1, d), kvmod3,
                lambda b: b * 2, lambda b: b * 2 + 1,
                [kv_w[:, :kv_width].astype(_BF16), kv_w[:, kv_width:].T.astype(_BF16)], [False, True],
                gate=(fgate_w.T.astype(_BF16), fgate_b.reshape(-1, 1)))
            nh = f_t.shape[1]
            f_sh = f_t.reshape(bsz, nh // 2, 2, s).transpose(0, 1, 3, 2)
        x = _ffn_call(x, mod3, mod_row, norm_g3, w_in, w_out, (layer, 0))
        g_spec = _resident((None, 1, d), lambda b, i, layer=layer: (layer * 3 + 1, 0, 0))
        if layer < n_a:
            w = a_w_qkv[layer]
            qt, k, vt = _proj_call(
                x, g_spec, norm_g3, mod3, lambda b: mod_row(b, 3), lambda b: mod_row(b, 4),
                [(w[:, :qk_w] * scale).T.astype(_BF16), w[:, qk_w:2 * qk_w].astype(_BF16),
                 w[:, 2 * qk_w:].T.astype(_BF16)], [True, False, True])
            o = _attn_call(qt, k, vt, diff=True, lambda_init=_lambda_init(layer),
                           extras=(rel_bias, a_lambda[layer], a_subln_g[layer].reshape(-1, 1)))
            mix = (o, a_wo, layer, 5)
        else:
            jb = layer - n_a
            (qt,) = _proj_call(
                x, g_spec, norm_g3, mod3, lambda b: mod_row(b, 3), lambda b: mod_row(b, 4),
                [(b_w_q[jb] * scale).T.astype(_BF16)], [True])
            o = _attn_call(qt, k_sh, vt_sh, diff=False, extras=(f_sh,))
            mix = (o, b_wo, jb, 5)
        x = _ffn_call(x, mod3, mod_row, norm_g3, w_in, w_out, (layer, 1), mix=mix,
                      final_g=final_g.reshape(1, d) if layer == depth - 1 else None)
    return x
```
